```python
import numpy as np
import jax
import jax.numpy as jnp
from jax import lax

D_MODEL = 2048
BATCH = 1
SEQ = 8192
DEPTH = 2
DEC_BATCH = 2
DEC_SEQ = 16384
PAST_LEN = 128

N_EVEN = (DEPTH + 1) // 2
N_ODD = DEPTH // 2

MIX_A = D_MODEL // 2
RWKV_HEAD = 64
RWKV_HEADS = MIX_A // RWKV_HEAD
DECAY_LORA = 64
ICL_LORA = 64
GATE_LORA = 128
RWKV_SPLITS = (MIX_A, MIX_A, MIX_A, DECAY_LORA, DECAY_LORA, ICL_LORA, ICL_LORA, GATE_LORA)
RWKV_COLS = sum(RWKV_SPLITS)

MIX_B = D_MODEL - MIX_A
ATT_HEAD = 128
ATT_Q_HEADS = MIX_B // ATT_HEAD
ATT_KV_HEADS = 2
ATT_GROUP = ATT_Q_HEADS // ATT_KV_HEADS
ATT_KV_WIDTH = ATT_KV_HEADS * ATT_HEAD
WINDOW = 128
ATT_BLOCK = 128
ROPE_THETA = 10000.0
PROJ_EVEN = RWKV_COLS + MIX_B + 2 * ATT_KV_WIDTH

MIX_C = D_MODEL // 2
CONV_C = 3
MIX_D = D_MODEL - MIX_C
CONV_D = 4
LRU_BLOCKS = 16
LRU_BLOCK = MIX_D // LRU_BLOCKS
LRU_C = 8.0
PROJ_ODD = 3 * MIX_C + 2 * MIX_D

FFN_DENSE = 5632
N_EXPERTS = 8
TOP_K = 2
FFN_EXPERT = 7168
MOE_BLOCK = 256

DN_ALPHA = (2 * DEPTH) ** 0.25
DN_BETA = (8 * DEPTH) ** -0.25
LN_EPS = 1e-5
GN_EPS = 64e-5
NEG_INF = -1e30

kernel_name = 'hybrid_bidir_rwkv7_swa_conv_rglru_encoder'


def _split(x, sizes):
    idx = np.cumsum(sizes)[:-1].tolist()
    return jnp.split(x, idx, axis=-1)


def layer_norm(x, g, b):
    xf = x.astype(jnp.float32)
    mu = xf.mean(-1, keepdims=True)
    var = jnp.mean(jnp.square(xf - mu), -1, keepdims=True)
    return ((xf - mu) * lax.rsqrt(var + LN_EPS) * g + b).astype(x.dtype)


def shift_prev(x):
    return jnp.pad(x, ((0, 0), (1, 0), (0, 0)))[:, :-1]


def shift_next(x):
    return jnp.pad(x, ((0, 0), (0, 1), (0, 0)))[:, 1:]


def rotary(x):
    T = x.shape[1]
    half = ATT_HEAD // 2
    inv_freq = ROPE_THETA ** (-jnp.arange(half, dtype=jnp.float32) / half)
    ang = jnp.arange(T, dtype=jnp.float32)[:, None] * inv_freq[None, :]
    cos = jnp.cos(ang)[None, :, None, :]
    sin = jnp.sin(ang)[None, :, None, :]
    xf = x.astype(jnp.float32)
    x1, x2 = xf[..., :half], xf[..., half:]
    return jnp.concatenate([x1 * cos - x2 * sin, x2 * cos + x1 * sin], -1).astype(x.dtype)


def swiglu(x, w_gu, w_down):
    g, u = jnp.split(x @ w_gu, 2, axis=-1)
    return (jax.nn.silu(g) * u) @ w_down


def depthwise_conv(u, w, pad):
    return lax.conv_general_dilated(u, w[:, None, :].astype(u.dtype), (1,), [pad],
                                    dimension_numbers=('NWC', 'WIO', 'NWC'),
                                    feature_group_count=u.shape[-1])


def rwkv_scan(r, w, k, v, a, b, reverse):
    bsz = r.shape[0]

    def step(S, inp):
        r_t, w_t, k_t, v_t, a_t, b_t = inp
        sa = jnp.einsum('bhij,bhj->bhi', S, a_t)
        S = S * w_t[:, :, None, :] + sa[..., None] * b_t[:, :, None, :] + v_t[..., None] * k_t[:, :, None, :]
        return S, jnp.einsum('bhij,bhj->bhi', S, r_t)

    xs = tuple(jnp.moveaxis(z, 1, 0) for z in (r, w, k, v, a, b))
    S0 = jnp.zeros((bsz, RWKV_HEADS, RWKV_HEAD, RWKV_HEAD), jnp.float32)
    _, y = lax.scan(step, S0, xs, reverse=reverse)
    return jnp.moveaxis(y, 0, 1)


def rwkv7_bidir(p, w0, w2, a0, a2, g2, k_k, k_a, r_k, gn_g, gn_b):
    bsz, T, _ = p.shape
    r, k, v, wd_f, wd_b, ad_f, ad_b, gd = _split(p, RWKV_SPLITS)

    def heads(z):
        return z.astype(jnp.float32).reshape(bsz, T, RWKV_HEADS, RWKV_HEAD)

    rh, vh = heads(r), heads(v)
    kk = heads(k * k_k)
    kk = kk * lax.rsqrt(jnp.maximum(jnp.sum(kk * kk, -1, keepdims=True), 1e-24))
    ys, bonuses = [], []
    for d, (wd, ad) in enumerate(((wd_f, ad_f), (wd_b, ad_b))):
        w_pre = (w0[d] + jnp.tanh(wd) @ w2[d]).astype(jnp.float32)
        decay = jnp.exp(-jnp.exp(-jax.nn.softplus(-w_pre) - 0.5))
        icl = jax.nn.sigmoid((a0[d] + ad @ a2[d]).astype(jnp.float32))
        kd = heads(k * (1.0 + (icl - 1.0) * k_a))
        ah = heads(icl)
        ys.append(rwkv_scan(rh, heads(decay), kd, vh, -kk, kk * ah, d == 1))
        bonuses.append(jnp.sum(rh * kd * r_k, -1, keepdims=True) * vh)
    y = ys[0] + ys[1]
    mu = y.mean(-1, keepdims=True)
    var = jnp.mean(jnp.square(y - mu), -1, keepdims=True)
    y = ((y - mu) * lax.rsqrt(var + GN_EPS)).reshape(bsz, T, MIX_A) * gn_g + gn_b
    y = y + (bonuses[0] + bonuses[1]).reshape(bsz, T, MIX_A)
    g = jax.nn.sigmoid(gd) @ g2
    return (y * g).astype(p.dtype)


def window_gqa(q, k, v, sink):
    bsz, T = q.shape[:2]
    nb = T // ATT_BLOCK
    q = rotary(q)
    k = rotary(k)
    qb = q.reshape(bsz, nb, ATT_BLOCK, ATT_KV_HEADS, ATT_GROUP, ATT_HEAD)

    def band(z):
        zp = jnp.pad(z, ((0, 0), (ATT_BLOCK, ATT_BLOCK), (0, 0), (0, 0)))
        zp = zp.reshape(bsz, nb + 2, ATT_BLOCK, ATT_KV_HEADS, ATT_HEAD)
        return jnp.concatenate([zp[:, :-2], zp[:, 1:-1], zp[:, 2:]], axis=2)

    kw, vw = band(k), band(v)
    s = jnp.einsum('bnqhgd,bnchd->bnhgqc', qb, kw,
                   preferred_element_type=jnp.float32) * (ATT_HEAD ** -0.5)
    qi = jnp.arange(ATT_BLOCK)[:, None]
    ci = jnp.arange(3 * ATT_BLOCK)[None, :]
    kpos = (jnp.arange(nb)[:, None, None] - 1) * ATT_BLOCK + ci[None]
    mask = (jnp.abs(ci - ATT_BLOCK - qi) <= WINDOW)[None] & (kpos >= 0) & (kpos < T)
    s = jnp.where(mask[None, :, None, None], s, NEG_INF)
    sk = sink.astype(jnp.float32).reshape(1, 1, ATT_KV_HEADS, ATT_GROUP, 1)
    m = jnp.maximum(s.max(-1), sk)
    p = jnp.exp(s - m[..., None])
    denom = p.sum(-1) + jnp.exp(sk - m)
    o = jnp.einsum('bnhgqc,bnchd->bnqhgd', p, vw.astype(jnp.float32))
    o = o / jnp.transpose(denom, (0, 1, 4, 2, 3))[..., None]
    return o.reshape(bsz, T, MIX_B).astype(q.dtype)


def _lin_combine(c1, c2):
    a1, b1 = c1
    a2, b2 = c2
    return a1 * a2, a2 * b1 + b2


def rglru(xc, wr, br, wi, bi, lam, reverse):
    bsz, T, _ = xc.shape
    xf = xc.astype(jnp.float32)
    xb = xf.reshape(bsz, T, LRU_BLOCKS, LRU_BLOCK)
    r = jax.nn.sigmoid(jnp.einsum('btni,nij->btnj', xb, wr).reshape(bsz, T, MIX_D) + br)
    i = jax.nn.sigmoid(jnp.einsum('btni,nij->btnj', xb, wi).reshape(bsz, T, MIX_D) + bi)
    log_a = -LRU_C * r * jax.nn.softplus(-lam.astype(jnp.float32))
    a = jnp.exp(log_a)
    b = jnp.sqrt(-jnp.expm1(2.0 * log_a)) * (i * xf)
    if reverse:
        a, b = jnp.flip(a, 1), jnp.flip(b, 1)
    _, h = lax.associative_scan(_lin_combine, (a, b), axis=1)
    if reverse:
        h = jnp.flip(h, 1)
    return h


def moe_swiglu(x, router, w_gu, w_down):
    n = x.shape[0]
    logits = jnp.dot(x, router, preferred_element_type=jnp.float32)
    top_logit, top_idx = lax.top_k(logits, TOP_K)
    gates = jax.nn.softmax(top_logit, axis=-1).astype(x.dtype)
    flat_e = top_idx.reshape(-1)
    flat_tok = jnp.repeat(jnp.arange(n, dtype=jnp.int32), TOP_K)
    flat_g = gates.reshape(-1)
    order = jnp.argsort(flat_e)
    se, stok, sg = flat_e[order], flat_tok[order], flat_g[order]
    counts = jnp.bincount(flat_e, length=N_EXPERTS)
    padded = (counts + MOE_BLOCK - 1) // MOE_BLOCK * MOE_BLOCK
    start = jnp.cumsum(counts) - counts
    pend = jnp.cumsum(padded)
    pstart = pend - padded
    dest = pstart[se] + (jnp.arange(n * TOP_K, dtype=jnp.int32) - start[se])
    n_blocks = -(-(n * TOP_K) // MOE_BLOCK) + N_EXPERTS
    slots = n_blocks * MOE_BLOCK
    slot_tok = jnp.full((slots,), n, jnp.int32).at[dest].set(stok)
    slot_gate = jnp.zeros((slots,), x.dtype).at[dest].set(sg)
    block_e = jnp.minimum(jnp.searchsorted(pend, jnp.arange(n_blocks) * MOE_BLOCK, side='right'),
                          N_EXPERTS - 1)
    x_pad = jnp.concatenate([x, jnp.zeros((1, x.shape[1]), x.dtype)], 0)
    xs = x_pad[slot_tok].reshape(n_blocks, MOE_BLOCK, x.shape[1])

    def expert_block(args):
        xb, e = args
        return swiglu(xb, w_gu[e], w_down[e])

    ys = lax.map(expert_block, (xs, block_e)).reshape(slots, x.shape[1])
    out = jax.ops.segment_sum(ys * slot_gate[:, None], slot_tok, num_segments=n + 1)
    return out[:n]


def even_layer(x, ln_g, ln_b, w_in, w_out, mu, w0, w2, a0, a2, g2, k_k, k_a, r_k,
               gn_g, gn_b, sink, ffn_gu, ffn_down):
    bsz, T, _ = x.shape
    proj = x @ w_in
    p_rw, q, k, v = _split(proj, (RWKV_COLS, MIX_B, ATT_KV_WIDTH, ATT_KV_WIDTH))
    p_rw = p_rw + mu[0] * (shift_prev(p_rw) - p_rw) + mu[1] * (shift_next(p_rw) - p_rw)
    y_a = rwkv7_bidir(p_rw, w0, w2, a0, a2, g2, k_k, k_a, r_k, gn_g, gn_b)
    y_b = window_gqa(q.reshape(bsz, T, ATT_Q_HEADS, ATT_HEAD),
                     k.reshape(bsz, T, ATT_KV_HEADS, ATT_HEAD),
                     v.reshape(bsz, T, ATT_KV_HEADS, ATT_HEAD), sink)
    mix = jnp.concatenate([y_a, y_b], -1) @ w_out
    x = layer_norm(DN_ALPHA * x + mix, ln_g[0], ln_b[0])
    x = layer_norm(DN_ALPHA * x + swiglu(x, ffn_gu, ffn_down), ln_g[1], ln_b[1])
    return x


def odd_layer(x, ln_g, ln_b, w_in, w_out, sc_w, conv_w, conv_b, wr, br, wi, bi, lam,
              router, e_gu, e_down):
    bsz, T, _ = x.shape
    proj = x @ w_in
    h, gb, gc, xr, gate = _split(proj, (MIX_C, MIX_C, MIX_C, MIX_D, MIX_D))
    y_c = gb * depthwise_conv(gc * h, sc_w, (CONV_C // 2, CONV_C // 2))
    xc = depthwise_conv(xr, conv_w, (CONV_D // 2, CONV_D - 1 - CONV_D // 2)) + conv_b
    hsum = (rglru(xc, wr[0], br[0], wi[0], bi[0], lam[0], False)
            + rglru(xc, wr[1], br[1], wi[1], bi[1], lam[1], True))
    y_d = (hsum * jax.nn.gelu(gate.astype(jnp.float32))).astype(x.dtype)
    mix = jnp.concatenate([y_c, y_d], -1) @ w_out
    x = layer_norm(DN_ALPHA * x + mix, ln_g[0], ln_b[0])
    ff = moe_swiglu(x.reshape(bsz * T, D_MODEL), router, e_gu, e_down).reshape(bsz, T, D_MODEL)
    x = layer_norm(DN_ALPHA * x + ff, ln_g[1], ln_b[1])
    return x


def setup_inputs(seed: int = 0) -> dict:
    key = jax.random.key(seed)
    ks = iter(jax.random.split(key, 48))
    f32 = jnp.float32

    def nrm(shape, scale):
        return jax.random.normal(next(ks), shape, f32) * scale

    def unif(shape, lo, hi):
        return jax.random.uniform(next(ks), shape, f32, lo, hi)

    lru_a = unif((N_ODD, 2, MIX_D), 0.9, 0.999) ** (1.0 / LRU_C)
    return {
        'x_prompt': nrm((BATCH, SEQ, D_MODEL), 1.0),
        'x_sample': nrm((DEC_BATCH, DEC_SEQ, D_MODEL), 1.0),
        'ln_g': 1.0 + nrm((DEPTH, 2, D_MODEL), 0.02),
        'ln_b': nrm((DEPTH, 2, D_MODEL), 0.02),
        'ev_w_in': nrm((N_EVEN, D_MODEL, PROJ_EVEN), D_MODEL ** -0.5),
        'ev_w_out': nrm((N_EVEN, D_MODEL, D_MODEL), DN_BETA * D_MODEL ** -0.5),
        'rw_mu': unif((N_EVEN, 2, RWKV_COLS), 0.0, 0.4),
        'rw_w0': unif((N_EVEN, 2, MIX_A), -6.0, -1.0),
        'rw_w2': nrm((N_EVEN, 2, DECAY_LORA, MIX_A), 0.1 * DECAY_LORA ** -0.5),
        'rw_a0': nrm((N_EVEN, 2, MIX_A), 0.1),
        'rw_a2': nrm((N_EVEN, 2, ICL_LORA, MIX_A), 0.3 * ICL_LORA ** -0.5),
        'rw_g2': nrm((N_EVEN, GATE_LORA, MIX_A), GATE_LORA ** -0.5),
        'rw_kk': 0.85 + nrm((N_EVEN, MIX_A), 0.05),
        'rw_ka': 1.0 + nrm((N_EVEN, MIX_A), 0.05),
        'rw_rk': nrm((N_EVEN, RWKV_HEADS, RWKV_HEAD), 0.1),
        'rw_gn_g': 1.0 + nrm((N_EVEN, MIX_A), 0.02),
        'rw_gn_b': nrm((N_EVEN, MIX_A), 0.02),
        'att_sink': nrm((N_EVEN, ATT_Q_HEADS), 0.5),
        'ffn_w_gu': nrm((N_EVEN, D_MODEL, 2 * FFN_DENSE), D_MODEL ** -0.5),
        'ffn_w_down': nrm((N_EVEN, FFN_DENSE, D_MODEL), DN_BETA * FFN_DENSE ** -0.5),
        'od_w_in': nrm((N_ODD, D_MODEL, PROJ_ODD), D_MODEL ** -0.5),
        'od_w_out': nrm((N_ODD, D_MODEL, D_MODEL), DN_BETA * D_MODEL ** -0.5),
        'sc_conv': nrm((N_ODD, CONV_C, MIX_C), CONV_C ** -0.5),
        'lru_conv': nrm((N_ODD, CONV_D, MIX_D), CONV_D ** -0.5),
        'lru_conv_b': nrm((N_ODD, MIX_D), 0.02),
        'lru_wr': nrm((N_ODD, 2, LRU_BLOCKS, LRU_BLOCK, LRU_BLOCK), LRU_BLOCK ** -0.5),
        'lru_br': nrm((N_ODD, 2, MIX_D), 0.1),
        'lru_wi': nrm((N_ODD, 2, LRU_BLOCKS, LRU_BLOCK, LRU_BLOCK), LRU_BLOCK ** -0.5),
        'lru_bi': nrm((N_ODD, 2, MIX_D), 0.1),
        'lru_lam': jnp.log(lru_a) - jnp.log1p(-lru_a),
        'moe_router': nrm((N_ODD, D_MODEL, N_EXPERTS), D_MODEL ** -0.5),
        'moe_w_gu': nrm((N_ODD, N_EXPERTS, D_MODEL, 2 * FFN_EXPERT), D_MODEL ** -0.5),
        'moe_w_down': nrm((N_ODD, N_EXPERTS, FFN_EXPERT, D_MODEL), DN_BETA * FFN_EXPERT ** -0.5),
    }


def reference(x_prompt, x_sample, ln_g, ln_b, ev_w_in, ev_w_out, rw_mu, rw_w0, rw_w2, rw_a0,
              rw_a2, rw_g2, rw_kk, rw_ka, rw_rk, rw_gn_g, rw_gn_b, att_sink, ffn_w_gu,
              ffn_w_down, od_w_in, od_w_out, sc_conv, lru_conv, lru_conv_b, lru_wr, lru_br,
              lru_wi, lru_bi, lru_lam, moe_router, moe_w_gu, moe_w_down):
    def encoder(x):
        for layer in range(DEPTH):
            i = layer // 2
            if layer % 2 == 0:
                x = even_layer(x, ln_g[layer], ln_b[layer], ev_w_in[i], ev_w_out[i], rw_mu[i],
                               rw_w0[i], rw_w2[i], rw_a0[i], rw_a2[i], rw_g2[i], rw_kk[i],
                               rw_ka[i], rw_rk[i], rw_gn_g[i], rw_gn_b[i], att_sink[i],
                               ffn_w_gu[i], ffn_w_down[i])
            else:
                x = odd_layer(x, ln_g[layer], ln_b[layer], od_w_in[i], od_w_out[i], sc_conv[i],
                              lru_conv[i], lru_conv_b[i], lru_wr[i], lru_br[i], lru_wi[i],
                              lru_bi[i], lru_lam[i], moe_router[i], moe_w_gu[i], moe_w_down[i])
        return x

    y_prompt = encoder(x_prompt)
    y_sample = encoder(x_sample)
    return (y_prompt, y_sample)
```

```python
import functools

import numpy as np
import jax
import jax.numpy as jnp
from jax import lax
from jax.experimental import pallas as pl
from jax.experimental.pallas import tpu as pltpu

F32 = jnp.float32
BF16 = jnp.bfloat16

D_MODEL = 2048
DEPTH = 2

MIX_A = D_MODEL // 2
RWKV_HEAD = 64
DECAY_LORA = 64
ICL_LORA = 64
GATE_LORA = 128
LORA_COLS = 2 * DECAY_LORA + 2 * ICL_LORA + GATE_LORA

MIX_B = D_MODEL - MIX_A
ATT_HEAD = 128
ATT_Q_HEADS = MIX_B // ATT_HEAD
ATT_KV_HEADS = 2
ATT_GROUP = ATT_Q_HEADS // ATT_KV_HEADS
ATT_KV_WIDTH = ATT_KV_HEADS * ATT_HEAD
WINDOW = 128
ATT_BLOCK = 128
ROPE_THETA = 10000.0

COL_R, COL_K, COL_V, COL_Q = 0, MIX_A, 2 * MIX_A, 3 * MIX_A
COL_AK = COL_Q + MIX_B
COL_AV = COL_AK + ATT_KV_WIDTH
COL_LORA = COL_AV + ATT_KV_WIDTH
PROJ_EVEN_PAD = 5120

MIX_C = D_MODEL // 2
MIX_D = D_MODEL - MIX_C
LRU_BLOCKS = 16
LRU_BLOCK = MIX_D // LRU_BLOCKS
LRU_C = 8.0

FFN_DENSE = 5632
N_EXPERTS = 8
TOP_K = 2
FFN_EXPERT = 7168

DN_ALPHA = (2 * DEPTH) ** 0.25
LN_EPS = 1e-5
GN_EPS = 64e-5
NEG_INF = -1e30

LANES = 128
SUBLANES = 8
VMEM_LIMIT = 56 * 1024 * 1024

HALO = SUBLANES
MOE_TM = 512


def _cparams(*sem):
    return pltpu.CompilerParams(dimension_semantics=sem, vmem_limit_bytes=VMEM_LIMIT)


def _sigmoid(x):
    return 1.0 / (1.0 + jnp.exp(-x))


def _layer_norm(z, g, b):
    mu = jnp.mean(z, axis=-1, keepdims=True)
    zc = z - mu
    var = jnp.mean(zc * zc, axis=-1, keepdims=True)
    return zc * lax.rsqrt(var + LN_EPS) * g + b


def _dot(a, b):
    return jnp.dot(a, b, preferred_element_type=F32)


def _split2(x):
    hi = x.astype(BF16)
    lo = (x - hi.astype(F32)).astype(BF16)
    return hi, lo


def _seg_sum(x, ones_bd):
    hi, lo = _split2(x)
    return _dot(hi, ones_bd) + _dot(lo, ones_bd)


def _seg_sum_wide(x, ones_bd):
    n = x.shape[1] // LANES
    return jnp.concatenate(
        [_seg_sum(x[:, g * LANES:(g + 1) * LANES], ones_bd) for g in range(n)], axis=1)


def _ones_bd():
    seg = np.arange(LANES) // RWKV_HEAD
    return jnp.asarray(seg[:, None] == seg[None, :], dtype=BF16)


def _mm_kernel(x_ref, w_ref, o_ref, xb_ref):
    @pl.when(pl.program_id(1) == 0)
    def _():
        xb_ref[...] = x_ref[...].astype(BF16)
    o_ref[...] = _dot(xb_ref[...], w_ref[...]).astype(o_ref.dtype)


def matmul(x, w, tm, tn, out_dtype=F32):
    m, k = x.shape
    n = w.shape[1]
    return pl.pallas_call(
        _mm_kernel,
        grid=(m // tm, n // tn),
        in_specs=[pl.BlockSpec((tm, k), lambda i, j: (i, 0)),
                  pl.BlockSpec((k, tn), lambda i, j: (0, j))],
        out_specs=pl.BlockSpec((tm, tn), lambda i, j: (i, j)),
        out_shape=jax.ShapeDtypeStruct((m, n), out_dtype),
        scratch_shapes=[pltpu.VMEM((tm, k), BF16)],
        compiler_params=_cparams("parallel", "arbitrary"),
        name="matmul",
    )(x, w)


def _mix_out_ln_kernel(xa_ref, xb_ref, wa_ref, wb_ref, res_ref, g_ref, b_ref, o_ref):
    mix = _dot(xa_ref[...], wa_ref[...]) + _dot(xb_ref[...], wb_ref[...])
    o_ref[...] = _layer_norm(DN_ALPHA * res_ref[...] + mix, g_ref[...], b_ref[...])


def mix_out_ln(xa, xb, w, res, g, b, tm):
    m = xa.shape[0]
    half = D_MODEL // 2
    return pl.pallas_call(
        _mix_out_ln_kernel,
        grid=(m // tm,),
        in_specs=[pl.BlockSpec((tm, half), lambda i: (i, 0)),
                  pl.BlockSpec((tm, half), lambda i: (i, 0)),
                  pl.BlockSpec((half, D_MODEL), lambda i: (0, 0)),
                  pl.BlockSpec((half, D_MODEL), lambda i: (1, 0)),
                  pl.BlockSpec((tm, D_MODEL), lambda i: (i, 0)),
                  pl.BlockSpec((1, D_MODEL), lambda i: (0, 0)),
                  pl.BlockSpec((1, D_MODEL), lambda i: (0, 0))],
        out_specs=pl.BlockSpec((tm, D_MODEL), lambda i: (i, 0)),
        out_shape=jax.ShapeDtypeStruct((m, D_MODEL), F32),
        compiler_params=_cparams("parallel"),
        name="mix_out_ln",
    )(xa, xb, w, w, res, g, b)


def _swiglu_up_kernel(x_ref, wg_ref, wu_ref, o_ref, xb_ref):
    @pl.when(pl.program_id(1) == 0)
    def _():
        xb_ref[...] = x_ref[...].astype(BF16)
    xb = xb_ref[...]
    g = _dot(xb, wg_ref[...])
    u = _dot(xb, wu_ref[...])
    o_ref[...] = (g * _sigmoid(g) * u).astype(o_ref.dtype)


def swiglu_up(x, w_gu, tm, tn):
    m, k = x.shape
    f = w_gu.shape[1] // 2
    nj = f // tn
    return pl.pallas_call(
        _swiglu_up_kernel,
        grid=(m // tm, nj),
        in_specs=[pl.BlockSpec((tm, k), lambda i, j: (i, 0)),
                  pl.BlockSpec((k, tn), lambda i, j: (0, j)),
                  pl.BlockSpec((k, tn), lambda i, j: (0, j + nj))],
        out_specs=pl.BlockSpec((tm, tn), lambda i, j: (i, j)),
        out_shape=jax.ShapeDtypeStruct((m, f), BF16),
        scratch_shapes=[pltpu.VMEM((tm, k), BF16)],
        compiler_params=_cparams("parallel", "arbitrary"),
        name="swiglu_up",
    )(x, w_gu, w_gu)


def _down_ln_kernel(h_ref, w_ref, res_ref, g_ref, b_ref, o_ref, acc_ref):
    kk = pl.program_id(1)

    @pl.when(kk == 0)
    def _():
        acc_ref[...] = jnp.zeros_like(acc_ref)

    acc_ref[...] += _dot(h_ref[...], w_ref[...])

    @pl.when(kk == pl.num_programs(1) - 1)
    def _():
        o_ref[...] = _layer_norm(DN_ALPHA * res_ref[...] + acc_ref[...], g_ref[...], b_ref[...])


def down_ln(h, w, res, g, b, tm, tk):
    m, f = h.shape
    return pl.pallas_call(
        _down_ln_kernel,
        grid=(m // tm, f // tk),
        in_specs=[pl.BlockSpec((tm, tk), lambda i, k: (i, k)),
                  pl.BlockSpec((tk, D_MODEL), lambda i, k: (k, 0)),
                  pl.BlockSpec((tm, D_MODEL), lambda i, k: (i, 0)),
                  pl.BlockSpec((1, D_MODEL), lambda i, k: (0, 0)),
                  pl.BlockSpec((1, D_MODEL), lambda i, k: (0, 0))],
        out_specs=pl.BlockSpec((tm, D_MODEL), lambda i, k: (i, 0)),
        out_shape=jax.ShapeDtypeStruct((m, D_MODEL), F32),
        scratch_shapes=[pltpu.VMEM((tm, D_MODEL), F32)],
        compiler_params=_cparams("parallel", "arbitrary"),
        name="down_ln",
    )(h, w, res, g, b)


def _halo_specs(width, col_block, tb, seq_blocks, total_rows):
    per = tb // HALO
    last = total_rows // HALO - 1

    def cur(b, i):
        return (b * seq_blocks + i, col_block)

    def prev(b, i):
        return (jnp.maximum((b * seq_blocks + i) * per - 1, 0), col_block)

    def nxt(b, i):
        return (jnp.minimum((b * seq_blocks + i + 1) * per, last), col_block)

    return [pl.BlockSpec((tb, width), cur),
            pl.BlockSpec((HALO, width), prev),
            pl.BlockSpec((HALO, width), nxt)]


def _shift_rows(x, halo_prev, halo_next, k):
    tb = x.shape[0]
    row = lax.broadcasted_iota(jnp.int32, x.shape, 0)
    if k > 0:
        out = pltpu.roll(x, k, axis=0)
        for m in range(k):
            out = jnp.where(row == m, halo_prev[HALO - k + m:HALO - k + m + 1, :], out)
        return out
    k = -k
    out = pltpu.roll(x, tb - k, axis=0)
    for m in range(k):
        out = jnp.where(row == tb - k + m, halo_next[m:m + 1, :], out)
    return out


def _edge_masks():
    i = pl.program_id(1)
    first = (i == 0).astype(F32)
    last = (i == pl.num_programs(1) - 1).astype(F32)
    return 1.0 - first, 1.0 - last


def _rwkv_prep_kernel(r_ref, rp_ref, rn_ref, k_ref, kp_ref, kn_ref, v_ref, vp_ref, vn_ref,
                      l_ref, lp_ref, ln_ref, mu_rkv_ref, mu_l_ref, w0_ref, w2_ref, a0_ref,
                      a2_ref, g2_ref, kk_ref, ka_ref, rk_ref, ones_ref,
                      r_o, v_o, kkn_o, dec_f_o, kd_f_o, icl_f_o, dec_b_o, kd_b_o, icl_b_o,
                      bonus_o, g_o):
    keep_p, keep_n = _edge_masks()
    ones_bd = ones_ref[...]

    def shifted(c_ref, p_ref, n_ref, mu):
        x = c_ref[...]
        xp = _shift_rows(x, p_ref[...] * keep_p, None, 1)
        xn = _shift_rows(x, None, n_ref[...] * keep_n, -1)
        return x + mu[0:1, :] * (xp - x) + mu[1:2, :] * (xn - x)

    mu_rkv = mu_rkv_ref[...]
    r = shifted(r_ref, rp_ref, rn_ref, mu_rkv[:, 0:MIX_A])
    k = shifted(k_ref, kp_ref, kn_ref, mu_rkv[:, MIX_A:2 * MIX_A])
    v = shifted(v_ref, vp_ref, vn_ref, mu_rkv[:, 2 * MIX_A:3 * MIX_A])
    lo = shifted(l_ref, lp_ref, ln_ref, mu_l_ref[...])

    kk = k * kk_ref[...]
    ss = _seg_sum_wide(kk * kk, ones_bd)
    kk = kk * lax.rsqrt(jnp.maximum(ss, 1e-24))

    wd = jnp.tanh(lo[:, 0:LANES]).astype(BF16)
    ad = lo[:, LANES:2 * LANES].astype(BF16)
    gd = _sigmoid(lo[:, 2 * LANES:3 * LANES]).astype(BF16)

    r_o[...] = r
    v_o[...] = v
    kkn_o[...] = kk
    rk = rk_ref[...]
    ka = ka_ref[...]
    bonus = jnp.zeros_like(r)
    for d, (dec_o, kd_o, icl_o) in enumerate(((dec_f_o, kd_f_o, icl_f_o),
                                              (dec_b_o, kd_b_o, icl_b_o))):
        w_pre = w0_ref[d:d + 1, :] + _dot(wd, w2_ref[d])
        dec_o[...] = jnp.exp(-np.float32(np.exp(-0.5)) * _sigmoid(w_pre))
        icl = _sigmoid(a0_ref[d:d + 1, :] + _dot(ad, a2_ref[d]))
        icl_o[...] = icl
        kd = k * (1.0 + (icl - 1.0) * ka)
        kd_o[...] = kd
        bonus = bonus + _seg_sum_wide(r * kd * rk, ones_bd) * v
    bonus_o[...] = bonus
    g_o[...] = _dot(gd, g2_ref[...])


def rwkv_prep(proj, params, bsz, seq, tb):
    n = bsz * seq
    nb = seq // tb
    specs = []
    for col in (COL_R // MIX_A, COL_K // MIX_A, COL_V // MIX_A):
        specs += _halo_specs(MIX_A, col, tb, nb, n)
    specs += _halo_specs(LORA_COLS, COL_LORA // LORA_COLS, tb, nb, n)

    def full(a):
        nd = a.ndim
        return pl.BlockSpec(a.shape, lambda b, i, _nd=nd: (0,) * _nd)

    specs += [full(p) for p in params]
    row = pl.BlockSpec((tb, MIX_A), lambda b, i: (b * nb + i, 0))
    outs = pl.pallas_call(
        _rwkv_prep_kernel,
        grid=(bsz, nb),
        in_specs=specs,
        out_specs=[row] * 11,
        out_shape=[jax.ShapeDtypeStruct((n, MIX_A), F32)] * 11,
        compiler_params=_cparams("parallel", "parallel"),
        name="rwkv_prep",
    )(*([proj] * 12), *params)
    return outs


def _rwkv_scan_kernel(rf_ref, vf_ref, kkf_ref, decf_ref, kdf_ref, iclf_ref,
                      rb_ref, vb_ref, kkb_ref, decb_ref, kdb_ref, iclb_ref, ones_ref,
                      yf_ref, yb_ref, s_ref):
    tb = rf_ref.shape[0]
    ngrp = MIX_A // LANES

    @pl.when(pl.program_id(1) == 0)
    def _():
        s_ref[...] = jnp.zeros_like(s_ref)

    ones_bd = ones_ref[...]
    lane = lax.broadcasted_iota(jnp.int32, (RWKV_HEAD, LANES), 1)
    sub = lax.broadcasted_iota(jnp.int32, (RWKV_HEAD, LANES), 0)
    diag = (lane % RWKV_HEAD) == sub

    def seg(x):
        return _seg_sum(x, ones_bd)

    def one_dir(d, refs, y_ref, t):
        r_ref, v_ref, kk_ref, dec_ref, kd_ref, icl_ref = refs
        r = r_ref[pl.ds(t, 1), :]
        v = v_ref[pl.ds(t, 1), :]
        kk = kk_ref[pl.ds(t, 1), :]
        w = dec_ref[pl.ds(t, 1), :]
        k = kd_ref[pl.ds(t, 1), :]
        a = -kk
        b = kk * icl_ref[pl.ds(t, 1), :]
        sl = [slice(g * LANES, (g + 1) * LANES) for g in range(ngrp)]
        s = [s_ref[d, g] for g in range(ngrp)]
        sa = seg(jnp.concatenate([s[g] * a[:, sl[g]] for g in range(ngrp)], axis=0))
        vcol = seg(jnp.concatenate(
            [jnp.where(diag, v[:, sl[g]], 0.0) for g in range(ngrp)], axis=0))
        q = []
        for g in range(ngrp):
            rows = slice(g * RWKV_HEAD, (g + 1) * RWKV_HEAD)
            s_new = s[g] * w[:, sl[g]] + sa[rows] * b[:, sl[g]] + vcol[rows] * k[:, sl[g]]
            s_ref[d, g] = s_new
            q.append(s_new * r[:, sl[g]])
        yb = seg(jnp.concatenate(q, axis=0))
        y = [jnp.sum(jnp.where(diag, yb[g * RWKV_HEAD:(g + 1) * RWKV_HEAD], 0.0),
                     axis=0, keepdims=True) for g in range(ngrp)]
        y_ref[pl.ds(t, 1), :] = jnp.concatenate(y, axis=1)

    fwd = (rf_ref, vf_ref, kkf_ref, decf_ref, kdf_ref, iclf_ref)
    bwd = (rb_ref, vb_ref, kkb_ref, decb_ref, kdb_ref, iclb_ref)

    def step(t, carry):
        one_dir(0, fwd, yf_ref, t)
        one_dir(1, bwd, yb_ref, tb - 1 - t)
        return carry

    lax.fori_loop(0, tb, step, 0)


def rwkv_scan(r, v, kk, dec_f, kd_f, icl_f, dec_b, kd_b, icl_b, bsz, seq, tb):
    n = bsz * seq
    nb = seq // tb
    fwd = pl.BlockSpec((tb, MIX_A), lambda b, i: (b * nb + i, 0))
    bwd = pl.BlockSpec((tb, MIX_A), lambda b, i: (b * nb + nb - 1 - i, 0))
    ones = _ones_bd()
    return pl.pallas_call(
        _rwkv_scan_kernel,
        grid=(bsz, nb),
        in_specs=[fwd] * 6 + [bwd] * 6 + [pl.BlockSpec((LANES, LANES), lambda b, i: (0, 0))],
        out_specs=[fwd, bwd],
        out_shape=[jax.ShapeDtypeStruct((n, MIX_A), F32)] * 2,
        scratch_shapes=[pltpu.VMEM((2, MIX_A // LANES, RWKV_HEAD, LANES), F32)],
        compiler_params=_cparams("parallel", "arbitrary"),
        name="rwkv_scan",
    )(r, v, kk, dec_f, kd_f, icl_f, r, v, kk, dec_b, kd_b, icl_b, ones)


def _rwkv_post_kernel(yf_ref, yb_ref, bonus_ref, g_ref, gn_g_ref, gn_b_ref, ones_ref, o_ref):
    ones_bd = ones_ref[...]
    y = yf_ref[...] + yb_ref[...]
    mu = _seg_sum_wide(y, ones_bd) * (1.0 / RWKV_HEAD)
    yc = y - mu
    var = _seg_sum_wide(yc * yc, ones_bd) * (1.0 / RWKV_HEAD)
    y = yc * lax.rsqrt(var + GN_EPS) * gn_g_ref[...] + gn_b_ref[...]
    o_ref[...] = ((y + bonus_ref[...]) * g_ref[...]).astype(o_ref.dtype)


def rwkv_post(y_f, y_b, bonus, g, gn_g, gn_b, tb):
    n = y_f.shape[0]
    row = pl.BlockSpec((tb, MIX_A), lambda i: (i, 0))
    vec = pl.BlockSpec((1, MIX_A), lambda i: (0, 0))
    return pl.pallas_call(
        _rwkv_post_kernel,
        grid=(n // tb,),
        in_specs=[row] * 4 + [vec, vec, pl.BlockSpec((LANES, LANES), lambda i: (0, 0))],
        out_specs=row,
        out_shape=jax.ShapeDtypeStruct((n, MIX_A), BF16),
        compiler_params=_cparams("parallel"),
        name="rwkv_post",
    )(y_f, y_b, bonus, g, gn_g, gn_b, _ones_bd())


def _rope_kernel(q_ref, k_ref, v_ref, cos_ref, sin_ref, qo_ref, ko_ref, vo_ref):
    cos = cos_ref[...]
    sin = sin_ref[...]
    half = ATT_HEAD // 2

    def rot(x):
        swapped = jnp.concatenate([x[:, half:], x[:, :half]], axis=1)
        return x * cos + swapped * sin

    scale = np.float32(ATT_HEAD ** -0.5)
    q = q_ref[...]
    qo_ref[...] = jnp.concatenate(
        [rot(q[:, h * ATT_HEAD:(h + 1) * ATT_HEAD]) * scale for h in range(ATT_Q_HEADS)],
        axis=1).astype(BF16)
    k = k_ref[...]
    ko_ref[...] = jnp.concatenate(
        [rot(k[:, h * ATT_HEAD:(h + 1) * ATT_HEAD]) for h in range(ATT_KV_HEADS)],
        axis=1).astype(BF16)
    vo_ref[...] = v_ref[...].astype(BF16)


def rope(proj, cos, sin, bsz, seq, tb):
    n = bsz * seq
    nb = seq // tb
    return pl.pallas_call(
        _rope_kernel,
        grid=(bsz, nb),
        in_specs=[pl.BlockSpec((tb, MIX_B), lambda b, i: (b * nb + i, COL_Q // MIX_B)),
                  pl.BlockSpec((tb, ATT_KV_WIDTH), lambda b, i: (b * nb + i, COL_AK // ATT_KV_WIDTH)),
                  pl.BlockSpec((tb, ATT_KV_WIDTH), lambda b, i: (b * nb + i, COL_AV // ATT_KV_WIDTH)),
                  pl.BlockSpec((tb, ATT_HEAD), lambda b, i: (i, 0)),
                  pl.BlockSpec((tb, ATT_HEAD), lambda b, i: (i, 0))],
        out_specs=[pl.BlockSpec((tb, MIX_B), lambda b, i: (b * nb + i, 0)),
                   pl.BlockSpec((tb, ATT_KV_WIDTH), lambda b, i: (b * nb + i, 0)),
                   pl.BlockSpec((tb, ATT_KV_WIDTH), lambda b, i: (b * nb + i, 0))],
        out_shape=[jax.ShapeDtypeStruct((n, MIX_B), BF16),
                   jax.ShapeDtypeStruct((n, ATT_KV_WIDTH), BF16),
                   jax.ShapeDtypeStruct((n, ATT_KV_WIDTH), BF16)],
        compiler_params=_cparams("parallel", "parallel"),
        name="rope",
    )(proj, proj, proj, cos, sin)


def _window_attn_kernel(q_ref, kp_ref, kc_ref, kn_ref, vp_ref, vc_ref, vn_ref, sink_ref, o_ref):
    i = pl.program_id(1)
    nb = pl.num_programs(1)
    blk = ATT_BLOCK
    kwin = jnp.concatenate([kp_ref[...], kc_ref[...], kn_ref[...]], axis=0)
    vwin = jnp.concatenate([vp_ref[...], vc_ref[...], vn_ref[...]], axis=0)
    qi = lax.broadcasted_iota(jnp.int32, (blk, 3 * blk), 0)
    ci = lax.broadcasted_iota(jnp.int32, (blk, 3 * blk), 1)
    mask = jnp.abs(ci - blk - qi) <= WINDOW
    mask = mask & ((ci >= blk) | (i > 0)) & ((ci < 2 * blk) | (i < nb - 1))
    mask = jnp.concatenate([mask] * ATT_GROUP, axis=0)
    q = q_ref[...]
    outs = []
    for hk in range(ATT_KV_HEADS):
        qs = jnp.concatenate(
            [q[:, (hk * ATT_GROUP + g) * ATT_HEAD:(hk * ATT_GROUP + g + 1) * ATT_HEAD]
             for g in range(ATT_GROUP)], axis=0)
        kh = kwin[:, hk * ATT_HEAD:(hk + 1) * ATT_HEAD]
        vh = vwin[:, hk * ATT_HEAD:(hk + 1) * ATT_HEAD]
        s = lax.dot_general(qs, kh, (((1,), (1,)), ((), ())), preferred_element_type=F32)
        s = jnp.where(mask, s, NEG_INF)
        sk = sink_ref[hk * ATT_GROUP * blk:(hk + 1) * ATT_GROUP * blk, 0:1]
        m = jnp.maximum(jnp.max(s, axis=-1, keepdims=True), sk)
        p = jnp.exp(s - m)
        denom = jnp.sum(p, axis=-1, keepdims=True) + jnp.exp(sk - m)
        o = _dot(p.astype(BF16), vh) / denom
        outs += [o[g * blk:(g + 1) * blk] for g in range(ATT_GROUP)]
    o_ref[...] = jnp.concatenate(outs, axis=1).astype(o_ref.dtype)


def window_attn(qr, kr, vb, sink_rows, bsz, seq):
    n = bsz * seq
    nb = seq // ATT_BLOCK
    last = n // ATT_BLOCK - 1

    def cur(b, i):
        return (b * nb + i, 0)

    def prev(b, i):
        return (jnp.maximum(b * nb + i - 1, 0), 0)

    def nxt(b, i):
        return (jnp.minimum(b * nb + i + 1, last), 0)

    kv = [pl.BlockSpec((ATT_BLOCK, ATT_KV_WIDTH), f) for f in (prev, cur, nxt)]
    return pl.pallas_call(
        _window_attn_kernel,
        grid=(bsz, nb),
        in_specs=[pl.BlockSpec((ATT_BLOCK, MIX_B), cur)] + kv + kv
                 + [pl.BlockSpec(sink_rows.shape, lambda b, i: (0, 0))],
        out_specs=pl.BlockSpec((ATT_BLOCK, MIX_B), cur),
        out_shape=jax.ShapeDtypeStruct((n, MIX_B), BF16),
        compiler_params=_cparams("parallel", "parallel"),
        name="window_attn",
    )(qr, kr, kr, kr, vb, vb, vb, sink_rows)


def _conv_kernel(h_ref, hp_ref, hn_ref, gb_ref, gc_ref, gcp_ref, gcn_ref,
                 xr_ref, xrp_ref, xrn_ref, scw_ref, cw_ref, cb_ref, yc_ref, xc_ref):
    keep_p, keep_n = _edge_masks()
    u = gc_ref[...] * h_ref[...]
    up = gcp_ref[...] * hp_ref[...] * keep_p
    un = gcn_ref[...] * hn_ref[...] * keep_n
    scw = scw_ref[...]
    conv = (scw[0:1, :] * _shift_rows(u, up, None, 1) + scw[1:2, :] * u
            + scw[2:3, :] * _shift_rows(u, None, un, -1))
    yc_ref[...] = (gb_ref[...] * conv).astype(yc_ref.dtype)
    x = xr_ref[...]
    xp = xrp_ref[...] * keep_p
    xn = xrn_ref[...] * keep_n
    cw = cw_ref[...]
    xc_ref[...] = (cw[0:1, :] * _shift_rows(x, xp, None, 2) + cw[1:2, :] * _shift_rows(x, xp, None, 1)
                   + cw[2:3, :] * x + cw[3:4, :] * _shift_rows(x, None, xn, -1) + cb_ref[...])


def short_convs(proj, sc_w, conv_w, conv_b, bsz, seq, tb):
    n = bsz * seq
    nb = seq // tb
    h3 = _halo_specs(MIX_C, 0, tb, nb, n)
    gb = _halo_specs(MIX_C, 1, tb, nb, n)[:1]
    gc3 = _halo_specs(MIX_C, 2, tb, nb, n)
    xr3 = _halo_specs(MIX_D, 3, tb, nb, n)

    def full(a):
        return pl.BlockSpec(a.shape, lambda b, i: (0, 0))

    row = pl.BlockSpec((tb, MIX_C), lambda b, i: (b * nb + i, 0))
    return pl.pallas_call(
        _conv_kernel,
        grid=(bsz, nb),
        in_specs=h3 + gb + gc3 + xr3 + [full(sc_w), full(conv_w), full(conv_b)],
        out_specs=[row, row],
        out_shape=[jax.ShapeDtypeStruct((n, MIX_C), BF16), jax.ShapeDtypeStruct((n, MIX_D), F32)],
        compiler_params=_cparams("parallel", "parallel"),
        name="short_convs",
    )(*([proj] * 10), sc_w, conv_w, conv_b)


def _rglru_kernel(*refs, reverse, final):
    if final:
        xc_ref, wr_ref, br_ref, wi_ref, bi_ref, lam_ref, ho_ref, gate_ref, o_ref, carry_ref = refs
    else:
        xc_ref, wr_ref, br_ref, wi_ref, bi_ref, lam_ref, o_ref, carry_ref = refs
    tb = xc_ref.shape[0]

    @pl.when(pl.program_id(1) == 0)
    def _():
        carry_ref[...] = jnp.zeros_like(carry_ref)

    x = xc_ref[...]
    xb = x.astype(BF16)
    npair = MIX_D // LANES

    def gate(w_ref, b_ref):
        z = jnp.concatenate(
            [_dot(xb[:, g * LANES:(g + 1) * LANES], w_ref[g]) for g in range(npair)], axis=1)
        return _sigmoid(z + b_ref[...])

    r = gate(wr_ref, br_ref)
    ig = gate(wi_ref, bi_ref)
    lam = lam_ref[...]
    softplus_neg = jnp.maximum(-lam, 0.0) + jnp.log1p(jnp.exp(-jnp.abs(lam)))
    log_a = -LRU_C * r * softplus_neg
    a = jnp.exp(log_a)
    b = jnp.sqrt(-jnp.tanh(log_a) * (a * a + 1.0)) * (ig * x)

    row = lax.broadcasted_iota(jnp.int32, a.shape, 0)
    s = 1
    while s < tb:
        if reverse:
            keep = row < tb - s
            a_sh = jnp.where(keep, pltpu.roll(a, tb - s, axis=0), 1.0)
            b_sh = jnp.where(keep, pltpu.roll(b, tb - s, axis=0), 0.0)
        else:
            keep = row >= s
            a_sh = jnp.where(keep, pltpu.roll(a, s, axis=0), 1.0)
            b_sh = jnp.where(keep, pltpu.roll(b, s, axis=0), 0.0)
        b = a * b_sh + b
        a = a * a_sh
        s *= 2
    h = b + a * carry_ref[...]
    carry_ref[...] = h[0:1, :] if reverse else h[tb - 1:tb, :]
    if final:
        gt = gate_ref[...]
        gelu = 0.5 * gt * (1.0 + jnp.tanh(np.float32(np.sqrt(2.0 / np.pi))
                                          * (gt + np.float32(0.044715) * gt * gt * gt)))
        o_ref[...] = ((h + ho_ref[...]) * gelu).astype(o_ref.dtype)
    else:
        o_ref[...] = h


def rglru(xc, wr, br, wi, bi, lam, bsz, seq, tb, reverse, other=None, proj=None):
    n = bsz * seq
    nb = seq // tb

    def blk(b, i):
        return (b * nb + (nb - 1 - i if reverse else i), 0)

    row = pl.BlockSpec((tb, MIX_D), blk)
    wspec = pl.BlockSpec(wr.shape, lambda b, i: (0, 0, 0))
    vec = pl.BlockSpec((1, MIX_D), lambda b, i: (0, 0))
    in_specs = [row, wspec, vec, wspec, vec, vec]
    args = [xc, wr, br, wi, bi, lam]
    final = other is not None
    if final:
        gate_col = 4 * MIX_C // MIX_D
        in_specs += [row, pl.BlockSpec((tb, MIX_D), lambda b, i: (blk(b, i)[0], gate_col))]
        args += [other, proj]
    return pl.pallas_call(
        functools.partial(_rglru_kernel, reverse=reverse, final=final),
        grid=(bsz, nb),
        in_specs=in_specs,
        out_specs=row,
        out_shape=jax.ShapeDtypeStruct((n, MIX_D), BF16 if final else F32),
        scratch_shapes=[pltpu.VMEM((1, MIX_D), F32)],
        compiler_params=_cparams("parallel", "arbitrary"),
        name="rglru_bwd" if reverse else "rglru_fwd",
    )(*args)


def _router_kernel(x_ref, rh_ref, rl_ref, idx_ref, gate_ref):
    x = x_ref[...]
    xh, xl = _split2(x)
    logits = _dot(xh, rh_ref[...]) + (_dot(xl, rh_ref[...]) + _dot(xh, rl_ref[...]))
    lane = lax.broadcasted_iota(jnp.int32, logits.shape, 1)
    logits = jnp.where(lane < N_EXPERTS, logits, -jnp.inf)
    m1 = jnp.max(logits, axis=-1, keepdims=True)
    i1 = jnp.min(jnp.where(logits == m1, lane, LANES), axis=-1, keepdims=True)
    rest = jnp.where(lane == i1, -jnp.inf, logits)
    m2 = jnp.max(rest, axis=-1, keepdims=True)
    i2 = jnp.min(jnp.where(rest == m2, lane, LANES), axis=-1, keepdims=True)
    e2 = jnp.exp(m2 - m1)
    g1 = 1.0 / (1.0 + e2)
    g2 = e2 / (1.0 + e2)
    idx_ref[...] = jnp.where(lane == 0, i1, jnp.where(lane == 1, i2, 0))
    gate_ref[...] = jnp.where(lane == 0, g1, jnp.where(lane == 1, g2, 0.0))


def router_top2(x, rh, rl, tm):
    n = x.shape[0]
    wide = pl.BlockSpec((tm, LANES), lambda i: (i, 0))
    return pl.pallas_call(
        _router_kernel,
        grid=(n // tm,),
        in_specs=[pl.BlockSpec((tm, D_MODEL), lambda i: (i, 0)),
                  pl.BlockSpec(rh.shape, lambda i: (0, 0)),
                  pl.BlockSpec(rl.shape, lambda i: (0, 0))],
        out_specs=[wide, wide],
        out_shape=[jax.ShapeDtypeStruct((n, LANES), jnp.int32),
                   jax.ShapeDtypeStruct((n, LANES), F32)],
        compiler_params=_cparams("parallel"),
        name="router_top2",
    )(x, rh, rl)


def _row_copy(src_hbm, dst_ref, src_row, dst_row, sem):
    return pltpu.make_async_copy(src_hbm.at[pl.ds(src_row, 1)], dst_ref.at[pl.ds(dst_row, 1)], sem)


def _gather_rows(idx_ref, idx_base, idx_stride, src_hbm, dst_ref, sem):
    rows = dst_ref.shape[0]

    def start(r, c):
        _row_copy(src_hbm, dst_ref, idx_ref[0, 0, idx_base + idx_stride * r], r, sem).start()
        return c

    def wait(r, c):
        _row_copy(src_hbm, dst_ref, 0, r, sem).wait()
        return c

    lax.fori_loop(0, rows, start, 0)
    lax.fori_loop(0, rows, wait, 0)


def _moe_up_kernel(be_ref, nv_ref, tok_ref, x_hbm, wg_ref, wu_ref, h_ref, xg_ref, xb_ref, sem):
    i = pl.program_id(0)
    j = pl.program_id(1)
    valid = i < nv_ref[0]

    @pl.when(valid & (j == 0))
    def _():
        _gather_rows(tok_ref, 0, 1, x_hbm, xg_ref, sem)
        xb_ref[...] = xg_ref[...].astype(BF16)

    @pl.when(valid)
    def _():
        xb = xb_ref[...]
        g = _dot(xb, wg_ref[0])
        u = _dot(xb, wu_ref[0])
        h_ref[...] = (g * _sigmoid(g) * u).astype(h_ref.dtype)

    @pl.when(jnp.logical_not(valid))
    def _():
        h_ref[...] = jnp.zeros_like(h_ref)


def moe_up(x, slot_tok, block_e, n_valid, w_gu, tn):
    nblk = block_e.shape[0]
    tm = MOE_TM
    f = w_gu.shape[2] // 2
    nj = f // tn

    def wmap(off):
        def f_(i, j, be, nv):
            jj = jnp.where(i < nv[0], j, nj - 1)
            return (be[i], 0, jj + off)
        return f_

    grid_spec = pltpu.PrefetchScalarGridSpec(
        num_scalar_prefetch=2,
        grid=(nblk, nj),
        in_specs=[pl.BlockSpec((1, 1, tm), lambda i, j, be, nv: (i, 0, 0), memory_space=pltpu.SMEM),
                  pl.BlockSpec(memory_space=pl.ANY),
                  pl.BlockSpec((1, D_MODEL, tn), wmap(0)),
                  pl.BlockSpec((1, D_MODEL, tn), wmap(nj))],
        out_specs=pl.BlockSpec((tm, tn), lambda i, j, be, nv: (i, j)),
        scratch_shapes=[pltpu.VMEM((tm, D_MODEL), F32), pltpu.VMEM((tm, D_MODEL), BF16),
                        pltpu.SemaphoreType.DMA(())],
    )
    return pl.pallas_call(
        _moe_up_kernel,
        grid_spec=grid_spec,
        out_shape=jax.ShapeDtypeStruct((nblk * tm, f), BF16),
        compiler_params=_cparams("arbitrary", "arbitrary"),
        name="moe_up",
    )(block_e, n_valid, slot_tok.reshape(nblk, 1, tm), x, w_gu, w_gu)


def _moe_down_kernel(be_ref, nv_ref, h_ref, w_ref, o_ref, acc_ref):
    i = pl.program_id(0)
    kk = pl.program_id(1)
    valid = i < nv_ref[0]

    @pl.when(kk == 0)
    def _():
        acc_ref[...] = jnp.zeros_like(acc_ref)

    @pl.when(valid)
    def _():
        acc_ref[...] += _dot(h_ref[...], w_ref[0])

    @pl.when(kk == pl.num_programs(1) - 1)
    def _():
        o_ref[...] = acc_ref[...]


def moe_down(h, block_e, n_valid, w_down, tk):
    nblk = block_e.shape[0]
    tm = MOE_TM
    f = h.shape[1]
    nk = f // tk

    def kidx(i, k, nv):
        return jnp.where(i < nv[0], k, nk - 1)

    grid_spec = pltpu.PrefetchScalarGridSpec(
        num_scalar_prefetch=2,
        grid=(nblk, nk),
        in_specs=[pl.BlockSpec((tm, tk), lambda i, k, be, nv: (i, kidx(i, k, nv))),
                  pl.BlockSpec((1, tk, D_MODEL), lambda i, k, be, nv: (be[i], kidx(i, k, nv), 0))],
        out_specs=pl.BlockSpec((tm, D_MODEL), lambda i, k, be, nv: (i, 0)),
        scratch_shapes=[pltpu.VMEM((tm, D_MODEL), F32)],
    )
    return pl.pallas_call(
        _moe_down_kernel,
        grid_spec=grid_spec,
        out_shape=jax.ShapeDtypeStruct((nblk * tm, D_MODEL), F32),
        compiler_params=_cparams("arbitrary", "arbitrary"),
        name="moe_down",
    )(block_e, n_valid, h, w_down)


def _moe_combine_ln_kernel(dest_ref, ys_hbm, gate_ref, res_ref, g_ref, b_ref, o_ref,
                           y0_ref, y1_ref, sem):
    _gather_rows(dest_ref, 0, TOP_K, ys_hbm, y0_ref, sem)
    _gather_rows(dest_ref, 1, TOP_K, ys_hbm, y1_ref, sem)
    gate = gate_ref[...]
    ff = gate[:, 0:1] * y0_ref[...] + gate[:, 1:2] * y1_ref[...]
    o_ref[...] = _layer_norm(DN_ALPHA * res_ref[...] + ff, g_ref[...], b_ref[...])


def moe_combine_ln(dest, ys, gates, res, g, b, tm):
    n = res.shape[0]
    return pl.pallas_call(
        _moe_combine_ln_kernel,
        grid=(n // tm,),
        in_specs=[pl.BlockSpec((1, 1, TOP_K * tm), lambda i: (i, 0, 0), memory_space=pltpu.SMEM),
                  pl.BlockSpec(memory_space=pl.ANY),
                  pl.BlockSpec((tm, LANES), lambda i: (i, 0)),
                  pl.BlockSpec((tm, D_MODEL), lambda i: (i, 0)),
                  pl.BlockSpec((1, D_MODEL), lambda i: (0, 0)),
                  pl.BlockSpec((1, D_MODEL), lambda i: (0, 0))],
        out_specs=pl.BlockSpec((tm, D_MODEL), lambda i: (i, 0)),
        out_shape=jax.ShapeDtypeStruct((n, D_MODEL), F32),
        scratch_shapes=[pltpu.VMEM((tm, D_MODEL), F32), pltpu.VMEM((tm, D_MODEL), F32),
                        pltpu.SemaphoreType.DMA(())],
        compiler_params=_cparams("arbitrary"),
        name="moe_combine_ln",
    )(dest.reshape(n // tm, 1, TOP_K * tm), ys, gates, res, g, b)


def moe_layer(x, res_ln, router_hl, w_gu, w_down):
    n = x.shape[0]
    ln_g, ln_b = res_ln
    idx, gates = router_top2(x, router_hl[0], router_hl[1], 512)
    flat_e = idx[:, :TOP_K].reshape(-1)
    onehot = (flat_e[:, None] == jnp.arange(N_EXPERTS, dtype=jnp.int32)[None, :]).astype(jnp.int32)
    csum = jnp.cumsum(onehot, axis=0)
    rank = jnp.take_along_axis(csum, flat_e[:, None], axis=1)[:, 0] - 1
    counts = csum[-1]
    padded = (counts + MOE_TM - 1) // MOE_TM * MOE_TM
    pend = jnp.cumsum(padded)
    pstart = pend - padded
    dest = (pstart[flat_e] + rank).astype(jnp.int32)
    nblk = (n * TOP_K) // MOE_TM + N_EXPERTS
    flat_tok = jnp.arange(n * TOP_K, dtype=jnp.int32) // TOP_K
    slot_tok = jnp.zeros((nblk * MOE_TM,), jnp.int32).at[dest].set(flat_tok)
    n_valid = (pend[-1] // MOE_TM).astype(jnp.int32)
    block_e = jnp.minimum(jnp.searchsorted(pend, jnp.arange(nblk, dtype=jnp.int32) * MOE_TM,
                                           side='right'), N_EXPERTS - 1).astype(jnp.int32)
    block_e = jnp.where(jnp.arange(nblk) < n_valid, block_e, block_e[n_valid - 1])
    nv = n_valid.reshape(1)
    h = moe_up(x, slot_tok, block_e, nv, w_gu, 512)
    ys = moe_down(h, block_e, nv, w_down, 512)
    return moe_combine_ln(dest, ys, gates, x, ln_g, ln_b, 256)


def _even_layer(x, bsz, seq, p):
    proj = matmul(x, p['w_in'], 1024, 512)
    tb = 256
    (r, v, kk, dec_f, kd_f, icl_f, dec_b, kd_b, icl_b, bonus, g) = rwkv_prep(
        proj, p['prep'], bsz, seq, tb)
    y_f, y_b = rwkv_scan(r, v, kk, dec_f, kd_f, icl_f, dec_b, kd_b, icl_b, bsz, seq, 128)
    y_a = rwkv_post(y_f, y_b, bonus, g, p['gn_g'], p['gn_b'], 512)
    qr, kr, vb = rope(proj, p['cos'][:seq], p['sin'][:seq], bsz, seq, 512)
    y_att = window_attn(qr, kr, vb, p['sink_rows'], bsz, seq)
    x = mix_out_ln(y_a, y_att, p['w_out'], x, p['ln_g'][0], p['ln_b'][0], 256)
    h = swiglu_up(x, p['ffn_gu'], 1024, 512)
    return down_ln(h, p['ffn_down'], x, p['ln_g'][1], p['ln_b'][1], 512, 512)


def _odd_layer(x, bsz, seq, p):
    proj = matmul(x, p['w_in'], 1024, 512)
    tb = 256
    y_c, xc = short_convs(proj, p['sc_w'], p['conv_w'], p['conv_b'], bsz, seq, tb)
    h_b = rglru(xc, *p['lru'][1], bsz, seq, tb, True)
    y_d = rglru(xc, *p['lru'][0], bsz, seq, tb, False, other=h_b, proj=proj)
    x = mix_out_ln(y_c, y_d, p['w_out'], x, p['ln_g'][0], p['ln_b'][0], 256)
    return moe_layer(x, (p['ln_g'][1], p['ln_b'][1]), p['router'], p['e_gu'], p['e_down'])


def _pad_lora(w, d):
    z = jnp.zeros_like(w)
    return jnp.concatenate([w, z] if d == 0 else [z, w], axis=0)


def _block_diag_pairs(w):
    w = w.reshape(LRU_BLOCKS // 2, 2, LRU_BLOCK, LRU_BLOCK)
    z = jnp.zeros_like(w[:, 0])
    top = jnp.concatenate([w[:, 0], z], axis=2)
    bot = jnp.concatenate([z, w[:, 1]], axis=2)
    return jnp.concatenate([top, bot], axis=1)


def _even_params(i, max_seq, ln_g, ln_b, ev_w_in, ev_w_out, rw_mu, rw_w0, rw_w2, rw_a0, rw_a2,
                 rw_g2, rw_kk, rw_ka, rw_rk, rw_gn_g, rw_gn_b, att_sink, ffn_w_gu, ffn_w_down):
    layer = 2 * i
    w = ev_w_in[i]
    rkv, lora = w[:, :3 * MIX_A], w[:, 3 * MIX_A:3 * MIX_A + LORA_COLS]
    qkv = w[:, 3 * MIX_A + LORA_COLS:]
    pad = jnp.zeros((D_MODEL, PROJ_EVEN_PAD - w.shape[1]), w.dtype)
    w_in = jnp.concatenate([rkv, qkv, lora, pad], axis=1).astype(BF16)
    mu = rw_mu[i]
    row = lambda a: a.reshape(1, -1)
    prep = (mu[:, :3 * MIX_A], mu[:, 3 * MIX_A:],
            rw_w0[i],
            jnp.stack([_pad_lora(rw_w2[i, d], d) for d in range(2)]).astype(BF16),
            rw_a0[i],
            jnp.stack([_pad_lora(rw_a2[i, d], d) for d in range(2)]).astype(BF16),
            rw_g2[i].astype(BF16), row(rw_kk[i]), row(rw_ka[i]), row(rw_rk[i]), _ones_bd())
    half = ATT_HEAD // 2
    inv_freq = ROPE_THETA ** (-jnp.arange(half, dtype=F32) / half)
    ang = jnp.arange(max_seq, dtype=F32)[:, None] * inv_freq[None, :]
    cos, sin = jnp.cos(ang), jnp.sin(ang)
    sink_rows = jnp.broadcast_to(
        jnp.repeat(att_sink[i].astype(F32), ATT_BLOCK)[:, None], (ATT_Q_HEADS * ATT_BLOCK, LANES))
    return dict(w_in=w_in, prep=prep, gn_g=row(rw_gn_g[i]), gn_b=row(rw_gn_b[i]),
                cos=jnp.concatenate([cos, cos], axis=1), sin=jnp.concatenate([-sin, sin], axis=1),
                sink_rows=sink_rows, w_out=ev_w_out[i].astype(BF16),
                ln_g=ln_g[layer][:, None, :], ln_b=ln_b[layer][:, None, :],
                ffn_gu=ffn_w_gu[i].astype(BF16), ffn_down=ffn_w_down[i].astype(BF16))


def _odd_params(i, ln_g, ln_b, od_w_in, od_w_out, sc_conv, lru_conv, lru_conv_b, lru_wr, lru_br,
                lru_wi, lru_bi, lru_lam, moe_router, moe_w_gu, moe_w_down):
    layer = 2 * i + 1
    row = lambda a: a.reshape(1, -1)
    lru = [(_block_diag_pairs(lru_wr[i, d]).astype(BF16), row(lru_br[i, d]),
            _block_diag_pairs(lru_wi[i, d]).astype(BF16), row(lru_bi[i, d]), row(lru_lam[i, d]))
           for d in range(2)]
    router = jnp.pad(moe_router[i], ((0, 0), (0, LANES - N_EXPERTS)))
    rh = router.astype(BF16)
    rl = (router - rh.astype(F32)).astype(BF16)
    return dict(w_in=od_w_in[i].astype(BF16), sc_w=sc_conv[i], conv_w=lru_conv[i],
                conv_b=row(lru_conv_b[i]), lru=lru, w_out=od_w_out[i].astype(BF16),
                ln_g=ln_g[layer][:, None, :], ln_b=ln_b[layer][:, None, :],
                router=(rh, rl), e_gu=moe_w_gu[i].astype(BF16), e_down=moe_w_down[i].astype(BF16))


def kernel(x_prompt, x_sample, ln_g, ln_b, ev_w_in, ev_w_out, rw_mu, rw_w0, rw_w2, rw_a0, rw_a2, rw_g2, rw_kk, rw_ka, rw_rk, rw_gn_g, rw_gn_b, att_sink, ffn_w_gu, ffn_w_down, od_w_in, od_w_out, sc_conv, lru_conv, lru_conv_b, lru_wr, lru_br, lru_wi, lru_bi, lru_lam, moe_router, moe_w_gu, moe_w_down):
    max_seq = max(x_prompt.shape[1], x_sample.shape[1])
    layers = []
    for layer in range(DEPTH):
        i = layer // 2
        if layer % 2 == 0:
            layers.append(_even_params(i, max_seq, ln_g, ln_b, ev_w_in, ev_w_out, rw_mu, rw_w0,
                                       rw_w2, rw_a0, rw_a2, rw_g2, rw_kk, rw_ka, rw_rk, rw_gn_g,
                                       rw_gn_b, att_sink, ffn_w_gu, ffn_w_down))
        else:
            layers.append(_odd_params(i, ln_g, ln_b, od_w_in, od_w_out, sc_conv, lru_conv,
                                      lru_conv_b, lru_wr, lru_br, lru_wi, lru_bi, lru_lam,
                                      moe_router, moe_w_gu, moe_w_down))

    def encoder(x):
        bsz, seq, _ = x.shape
        x = x.reshape(bsz * seq, D_MODEL)
        for layer, p in enumerate(layers):
            x = _even_layer(x, bsz, seq, p) if layer % 2 == 0 else _odd_layer(x, bsz, seq, p)
        return x.reshape(bsz, seq, D_MODEL)

    return (encoder(x_prompt), encoder(x_sample))
```
